```python
import math
import jax, jax.numpy as jnp
from jax import lax
import numpy as np

D_MODEL = 1024
BATCH = 8
SEQ = 4096
DEPTH = 4

GRID_W = 64
Q_BLOCK = 128
NORM_EPS = 1e-6
D_FF = 4 * D_MODEL
MLA_HEADS = 8
MLA_NOPE = 64
MLA_ROPE = 32
MLA_V = 64
MLA_Q_RANK = 384
MLA_KV_RANK = 256
ROPE_THETA = 10000.0
NA_HEADS = 8
NA_HEAD_DIM = 64
NA_ROWS = 8
NA_COLS = 16
NA_W = NA_HEADS * NA_HEAD_DIM
DIFF_HEADS = 8
DIFF_HEAD_DIM = 64
DIFF_W = DIFF_HEADS * 2 * DIFF_HEAD_DIM
T5_BUCKETS = 32
T5_MAX_DIST = 128

AB_IN = MLA_Q_RANK + MLA_KV_RANK + MLA_ROPE + 3 * NA_W
AB_OUT = MLA_HEADS * MLA_V + NA_W
C_IN = 3 * DIFF_W
N_EVEN = (DEPTH + 1) // 2
N_ODD = DEPTH // 2

kernel_name = "hybrid_mla_natten_diffattn_encoder"


def _rmsnorm(x, g):
    xf = x.astype(jnp.float32)
    y = xf * lax.rsqrt(jnp.mean(xf * xf, axis=-1, keepdims=True) + NORM_EPS)
    return (y * g.astype(jnp.float32)).astype(x.dtype)


def _rope_tables(S):
    inv_freq = ROPE_THETA ** (-jnp.arange(0, MLA_ROPE, 2, dtype=jnp.float32) / MLA_ROPE)
    ang = jnp.arange(S, dtype=jnp.float32)[:, None] * inv_freq[None, :]
    return jnp.cos(ang), jnp.sin(ang)


def _rope(x, cos, sin):
    c = cos[None, :, None, :].astype(x.dtype)
    s = sin[None, :, None, :].astype(x.dtype)
    x1, x2 = jnp.split(x, 2, axis=-1)
    return jnp.concatenate([x1 * c - x2 * s, x2 * c + x1 * s], axis=-1)


def _t5_bucket(rel):
    nb = T5_BUCKETS // 2
    max_exact = nb // 2
    ret = jnp.where(rel > 0, nb, 0)
    n = jnp.abs(rel)
    nf = jnp.maximum(n, 1).astype(jnp.float32)
    large = max_exact + (jnp.log(nf / max_exact) / math.log(T5_MAX_DIST / max_exact)
                         * (nb - max_exact)).astype(jnp.int32)
    large = jnp.minimum(large, nb - 1)
    return ret + jnp.where(n < max_exact, n, large)


def _dense_attention(q, k, v):
    B, S, H, Dq = q.shape
    nb = S // Q_BLOCK
    qb = q.reshape(B, nb, Q_BLOCK, H, Dq).swapaxes(0, 1)
    scale = Dq ** -0.5

    def one(qblk):
        s = jnp.einsum('bqhd,bkhd->bhqk', qblk, k).astype(jnp.float32) * scale
        p = jax.nn.softmax(s, axis=-1)
        return jnp.einsum('bhqk,bkhd->bqhd', p.astype(v.dtype), v)

    out = lax.map(one, qb)
    return out.swapaxes(0, 1).reshape(B, S, H, v.shape[-1])


def _neighborhood_attention(q, k, v, rpb):
    B, S, H, D = q.shape
    rows = S // GRID_W
    kh = min(NA_ROWS, rows)
    qg = q.reshape(B, rows, GRID_W, H, D)
    kg = k.reshape(B, rows, GRID_W, H, D)
    vg = v.reshape(B, rows, GRID_W, H, D)
    cols = np.arange(GRID_W)
    col_start = np.clip(cols - NA_COLS // 2, 0, GRID_W - NA_COLS)
    col_idx = col_start[:, None] + np.arange(NA_COLS)[None, :]
    dc = col_idx - cols[:, None] + NA_COLS - 1
    rpb_cols = rpb[:, :, dc]
    scale = D ** -0.5

    def one(r):
        rs = jnp.clip(r - kh // 2, 0, rows - kh)
        k_rows = lax.dynamic_slice_in_dim(kg, rs, kh, axis=1)
        v_rows = lax.dynamic_slice_in_dim(vg, rs, kh, axis=1)
        k_win = k_rows[:, :, col_idx]
        v_win = v_rows[:, :, col_idx]
        q_row = lax.dynamic_index_in_dim(qg, r, axis=1, keepdims=False)
        s = jnp.einsum('bwhd,bkwjhd->bhwkj', q_row, k_win).astype(jnp.float32) * scale
        dr = rs + jnp.arange(kh) - r + NA_ROWS - 1
        bias = jnp.take(rpb_cols, dr, axis=1).astype(jnp.float32)
        s = s + jnp.transpose(bias, (0, 2, 1, 3))[None]
        p = jax.nn.softmax(s.reshape(B, H, GRID_W, kh * NA_COLS), axis=-1)
        p = p.reshape(B, H, GRID_W, kh, NA_COLS).astype(v.dtype)
        return jnp.einsum('bhwkj,bkwjhd->bwhd', p, v_win)

    out = lax.map(one, jnp.arange(rows))
    return out.swapaxes(0, 1).reshape(B, S, H, D)


def _diff_attention(q1, q2, k1, k2, v, lam, t5_table):
    B, S, H, D = q1.shape
    nb = S // Q_BLOCK
    kpos = jnp.arange(S)
    scale = D ** -0.5

    def blocks(t):
        return t.reshape(B, nb, Q_BLOCK, H, D).swapaxes(0, 1)

    def one(args):
        q1b, q2b, blk = args
        qpos = blk * Q_BLOCK + jnp.arange(Q_BLOCK)
        bias = t5_table[_t5_bucket(kpos[None, :] - qpos[:, None])].astype(jnp.float32)
        bias = jnp.transpose(bias, (2, 0, 1))[None]
        s1 = jnp.einsum('bqhd,bkhd->bhqk', q1b, k1).astype(jnp.float32) * scale + bias
        s2 = jnp.einsum('bqhd,bkhd->bhqk', q2b, k2).astype(jnp.float32) * scale + bias
        p = jax.nn.softmax(s1, axis=-1) - lam * jax.nn.softmax(s2, axis=-1)
        return jnp.einsum('bhqk,bkhd->bqhd', p.astype(v.dtype), v)

    out = lax.map(one, (blocks(q1), blocks(q2), jnp.arange(nb)))
    return out.swapaxes(0, 1).reshape(B, S, H, v.shape[-1])


def _mla_natten_mixer(h, w_in, q_norm, w_uq, kv_norm, w_ukv, rpb, w_out, cos, sin):
    B, S, _ = h.shape
    proj = h @ w_in
    offs = np.cumsum([MLA_Q_RANK, MLA_KV_RANK, MLA_ROPE, NA_W, NA_W]).tolist()
    c_q, c_kv, k_pe, q_na, k_na, v_na = jnp.split(proj, offs, axis=-1)
    q = (_rmsnorm(c_q, q_norm) @ w_uq).reshape(B, S, MLA_HEADS, MLA_NOPE + MLA_ROPE)
    q_nope, q_pe = q[..., :MLA_NOPE], q[..., MLA_NOPE:]
    kv = (_rmsnorm(c_kv, kv_norm) @ w_ukv).reshape(B, S, MLA_HEADS, MLA_NOPE + MLA_V)
    k_nope, v = kv[..., :MLA_NOPE], kv[..., MLA_NOPE:]
    q_pe = _rope(q_pe, cos, sin)
    k_pe = jnp.broadcast_to(_rope(k_pe[:, :, None, :], cos, sin), (B, S, MLA_HEADS, MLA_ROPE))
    o_a = _dense_attention(jnp.concatenate([q_nope, q_pe], axis=-1),
                           jnp.concatenate([k_nope, k_pe], axis=-1), v)
    shp = (B, S, NA_HEADS, NA_HEAD_DIM)
    o_b = _neighborhood_attention(q_na.reshape(shp), k_na.reshape(shp), v_na.reshape(shp), rpb)
    o = jnp.concatenate([o_a.reshape(B, S, MLA_HEADS * MLA_V), o_b.reshape(B, S, NA_W)], axis=-1)
    return o @ w_out


def _diff_mixer(h, w_in, lq1, lk1, lq2, lk2, subln, w_out, t5_table, lambda_init):
    B, S, _ = h.shape
    q, k, v = jnp.split(h @ w_in, [DIFF_W, 2 * DIFF_W], axis=-1)
    q = q.reshape(B, S, DIFF_HEADS, 2, DIFF_HEAD_DIM)
    k = k.reshape(B, S, DIFF_HEADS, 2, DIFF_HEAD_DIM)
    v = v.reshape(B, S, DIFF_HEADS, 2 * DIFF_HEAD_DIM)
    f32 = jnp.float32
    lam = (jnp.exp(jnp.sum(lq1.astype(f32) * lk1.astype(f32)))
           - jnp.exp(jnp.sum(lq2.astype(f32) * lk2.astype(f32))) + lambda_init)
    o = _diff_attention(q[..., 0, :], q[..., 1, :], k[..., 0, :], k[..., 1, :], v, lam, t5_table)
    o = _rmsnorm(o, subln) * (1.0 - lambda_init)
    return o.reshape(B, S, DIFF_W) @ w_out


def _sq_relu_mlp(h, w1, w2):
    a = jax.nn.relu(h @ w1)
    return (a * a) @ w2


def _normal(key, shape, scale):
    return scale * jax.random.normal(key, shape, jnp.float32)


def setup_inputs(seed: int = 0) -> dict:
    key = jax.random.key(seed)
    ks = jax.random.split(key, 24)
    D = D_MODEL
    return {
        "x": _normal(ks[0], (BATCH, SEQ, D), 1.0),
        "norm_attn": 1.0 + _normal(ks[1], (DEPTH, D), 0.02),
        "norm_mlp": 1.0 + _normal(ks[2], (DEPTH, D), 0.02),
        "ab_w_in": _normal(ks[3], (N_EVEN, D, AB_IN), D ** -0.5),
        "ab_q_norm": 1.0 + _normal(ks[4], (N_EVEN, MLA_Q_RANK), 0.02),
        "ab_w_uq": _normal(ks[5], (N_EVEN, MLA_Q_RANK, MLA_HEADS * (MLA_NOPE + MLA_ROPE)), MLA_Q_RANK ** -0.5),
        "ab_kv_norm": 1.0 + _normal(ks[6], (N_EVEN, MLA_KV_RANK), 0.02),
        "ab_w_ukv": _normal(ks[7], (N_EVEN, MLA_KV_RANK, MLA_HEADS * (MLA_NOPE + MLA_V)), MLA_KV_RANK ** -0.5),
        "ab_natten_rpb": _normal(ks[8], (N_EVEN, NA_HEADS, 2 * NA_ROWS - 1, 2 * NA_COLS - 1), 0.1),
        "ab_w_out": _normal(ks[9], (N_EVEN, AB_OUT, D), AB_OUT ** -0.5),
        "c_w_in": _normal(ks[10], (N_ODD, D, C_IN), D ** -0.5),
        "c_lambda_q1": _normal(ks[11], (N_ODD, DIFF_HEAD_DIM), 0.1),
        "c_lambda_k1": _normal(ks[12], (N_ODD, DIFF_HEAD_DIM), 0.1),
        "c_lambda_q2": _normal(ks[13], (N_ODD, DIFF_HEAD_DIM), 0.1),
        "c_lambda_k2": _normal(ks[14], (N_ODD, DIFF_HEAD_DIM), 0.1),
        "c_subln": 1.0 + _normal(ks[15], (N_ODD, 2 * DIFF_HEAD_DIM), 0.02),
        "c_w_out": _normal(ks[16], (N_ODD, DIFF_W, D), DIFF_W ** -0.5),
        "t5_bias": _normal(ks[17], (T5_BUCKETS, DIFF_HEADS), 0.5),
        "mlp_w1": _normal(ks[18], (DEPTH, D, D_FF), D ** -0.5),
        "mlp_w2": _normal(ks[19], (DEPTH, D_FF, D), D_FF ** -0.5),
        "final_norm": 1.0 + _normal(ks[20], (D,), 0.02),
    }


def reference(x, norm_attn, norm_mlp, ab_w_in, ab_q_norm, ab_w_uq, ab_kv_norm, ab_w_ukv,
              ab_natten_rpb, ab_w_out, c_w_in, c_lambda_q1, c_lambda_k1, c_lambda_q2,
              c_lambda_k2, c_subln, c_w_out, t5_bias, mlp_w1, mlp_w2, final_norm):
    S = x.shape[1]
    cos, sin = _rope_tables(S)
    for layer in range(DEPTH):
        i = layer // 2
        h = _rmsnorm(x, norm_attn[layer])
        if layer % 2 == 0:
            mix = _mla_natten_mixer(h, ab_w_in[i], ab_q_norm[i], ab_w_uq[i], ab_kv_norm[i],
                                    ab_w_ukv[i], ab_natten_rpb[i], ab_w_out[i], cos, sin)
        else:
            lambda_init = 0.8 - 0.6 * math.exp(-0.3 * layer)
            mix = _diff_mixer(h, c_w_in[i], c_lambda_q1[i], c_lambda_k1[i], c_lambda_q2[i],
                              c_lambda_k2[i], c_subln[i], c_w_out[i], t5_bias, lambda_init)
        x = x + mix
        h = _rmsnorm(x, norm_mlp[layer])
        x = x + _sq_relu_mlp(h, mlp_w1[layer], mlp_w2[layer])
    return _rmsnorm(x, final_norm)
```

```python
import functools
import math

import numpy as np
import jax
import jax.numpy as jnp
from jax import lax
from jax.experimental import pallas as pl
from jax.experimental.pallas import tpu as pltpu

F32 = jnp.float32
BF16 = jnp.bfloat16

GRID_W = 64
NORM_EPS = 1e-6
MLA_HEADS = 8
MLA_NOPE = 64
MLA_ROPE = 32
MLA_V = 64
MLA_Q_RANK = 384
MLA_KV_RANK = 256
ROPE_THETA = 10000.0
NA_HEADS = 8
NA_HEAD_DIM = 64
NA_ROWS = 8
NA_COLS = 16
NA_W = NA_HEADS * NA_HEAD_DIM
DIFF_HEADS = 8
DIFF_HEAD_DIM = 64
DIFF_W = DIFF_HEADS * 2 * DIFF_HEAD_DIM
T5_BUCKETS = 32
T5_MAX_DIST = 128

LANES = 128
LOG2E = 1.4426950408889634
NEG_MASK = -1e30
VMEM_LIMIT = 56 * 1024 * 1024

NA_GROUP = 4
NA_WIN = NA_GROUP + NA_ROWS

_OFF_CQ = 0
_OFF_CKV = _OFF_CQ + MLA_Q_RANK
_OFF_QNA = _OFF_CKV + MLA_KV_RANK
_OFF_KNA = _OFF_QNA + NA_W
_OFF_VNA = _OFF_KNA + NA_W
_OFF_KPE = _OFF_VNA + NA_W
_OFF_KPESW = _OFF_KPE + LANES
AB_IN_PACKED = _OFF_KPESW + LANES


def _params(sem):
    return pltpu.CompilerParams(dimension_semantics=sem, vmem_limit_bytes=VMEM_LIMIT)


def _rms(xf, g):
    return xf * lax.rsqrt(jnp.mean(xf * xf, axis=-1, keepdims=True) + NORM_EPS) * g


def _dot(a, b):
    return jnp.dot(a, b, preferred_element_type=F32)


def _dot_nt(a, b):
    return lax.dot_general(a, b, (((1,), (1,)), ((), ())), preferred_element_type=F32)


def _ab_proj_body(x_ref, g_ref, win_ref, qn_ref, wuq_ref, kvn_ref, wukv_ref, qa_ref, qb_ref, kc_ref, ks_ref,
                  q_ref, k_ref, v_ref, qna_ref, kna_ref, vna_ref):
    h = _rms(x_ref[0], g_ref[...]).astype(BF16)

    def proj(a, b):
        return _dot(h, win_ref[:, a:b])

    qna_ref[0] = (proj(_OFF_QNA, _OFF_KNA) * (NA_HEAD_DIM ** -0.5 * LOG2E)).astype(BF16)
    kna_ref[0] = proj(_OFF_KNA, _OFF_VNA).astype(BF16)
    vna_ref[0] = proj(_OFF_VNA, _OFF_KPE).astype(BF16)

    cq = _rms(proj(_OFF_CQ, _OFF_CKV), qn_ref[...]).astype(BF16)
    ckv = _rms(proj(_OFF_CKV, _OFF_QNA), kvn_ref[...]).astype(BF16)
    k_pe = proj(_OFF_KPE, _OFF_KPESW) * kc_ref[...] + proj(_OFF_KPESW, AB_IN_PACKED) * ks_ref[...]
    hw = MLA_HEADS * LANES
    q_main = _dot(cq, wuq_ref[:, :hw])
    q_swap = _dot(cq, wuq_ref[:, hw:])
    k_main = _dot(ckv, wukv_ref[:, :hw])
    qa = qa_ref[...]
    qb = qb_ref[...]
    for hd in range(MLA_HEADS):
        sl = slice(hd * LANES, (hd + 1) * LANES)
        q_ref[0, :, sl] = (q_main[:, sl] * qa + q_swap[:, sl] * qb).astype(BF16)
        k_ref[0, :, sl] = (k_main[:, sl] + k_pe).astype(BF16)
    v_ref[0] = _dot(ckv, wukv_ref[:, hw:]).astype(BF16)


def _ab_proj(x, g, win, qn, wuq, kvn, wukv, qa, qb, kc, ks, tm):
    B, S, D = x.shape
    hw = MLA_HEADS * LANES
    vw = MLA_HEADS * MLA_V
    full = lambda a: pl.BlockSpec(a.shape, lambda b, i: (0,) * a.ndim)
    rows = lambda w: pl.BlockSpec((1, tm, w), lambda b, i: (b, i, 0))
    tab = pl.BlockSpec((tm, LANES), lambda b, i: (i, 0))
    out_w = (hw, hw, vw, NA_W, NA_W, NA_W)
    return pl.pallas_call(
        _ab_proj_body,
        grid=(B, S // tm),
        in_specs=[rows(D), full(g), full(win), full(qn), full(wuq), full(kvn), full(wukv), tab, tab, tab, tab],
        out_specs=[rows(w) for w in out_w],
        out_shape=[jax.ShapeDtypeStruct((B, S, w), BF16) for w in out_w],
        compiler_params=_params(("parallel", "parallel")),
        name="ab_proj",
    )(x, g, win, qn, wuq, kvn, wukv, qa, qb, kc, ks)


def _softmax_step(s, v, m, l, acc):
    m_new = jnp.maximum(m, jnp.max(s, axis=-1, keepdims=True))
    alpha = jnp.exp2(m - m_new)
    p = jnp.exp2(s - m_new)
    l = alpha * l + jnp.sum(p, axis=-1, keepdims=True)
    acc = alpha * acc + _dot(p.astype(BF16), v)
    return m_new, l, acc


def _softmax_init(tq):
    return (jnp.full((tq, 1), -jnp.inf, F32), jnp.zeros((tq, 1), F32), jnp.zeros((tq, LANES), F32))


def _mla_attn_body(q_ref, k_ref, v_ref, o_ref, *, tk):
    tq = q_ref.shape[1]
    nk = k_ref.shape[1] // tk
    qs = [q_ref[0, :, hd * LANES:(hd + 1) * LANES] for hd in range(2)]

    def step(j, carry):
        ks = pl.multiple_of(j * tk, tk)
        v = v_ref[0, pl.ds(ks, tk), :]
        out = []
        for hd in range(2):
            k = k_ref[0, pl.ds(ks, tk), hd * LANES:(hd + 1) * LANES]
            out.append(_softmax_step(_dot_nt(qs[hd], k), v, *carry[hd]))
        return tuple(out)

    res = lax.fori_loop(0, nk, step, (_softmax_init(tq), _softmax_init(tq)))
    o0 = res[0][2] / res[0][1]
    o1 = res[1][2] / res[1][1]
    lane = lax.broadcasted_iota(jnp.int32, (1, LANES), 1)
    o_ref[0] = jnp.where(lane < MLA_V, o0, o1).astype(o_ref.dtype)


def _mla_attn(q, k, v, tq, tk):
    B, S, _ = q.shape
    return pl.pallas_call(
        functools.partial(_mla_attn_body, tk=tk),
        grid=(B, MLA_HEADS // 2, S // tq),
        in_specs=[pl.BlockSpec((1, tq, 2 * LANES), lambda b, p, i: (b, i, p)),
                  pl.BlockSpec((1, S, 2 * LANES), lambda b, p, i: (b, 0, p)),
                  pl.BlockSpec((1, S, LANES), lambda b, p, i: (b, 0, p))],
        out_specs=pl.BlockSpec((1, tq, LANES), lambda b, p, i: (b, i, p)),
        out_shape=jax.ShapeDtypeStruct((B, S, MLA_HEADS * MLA_V), BF16),
        compiler_params=_params(("parallel", "parallel", "parallel")),
        name="mla_attn",
    )(q, k, v)


def _na_attn_body(q_ref, k_ref, v_ref, bias_ref, o_ref, *, n_groups):
    g = pl.program_id(2)
    max_ws = n_groups * NA_GROUP - NA_WIN
    ws = jnp.clip(g * NA_GROUP - NA_ROWS // 2, 0, max_ws)
    ks = pl.multiple_of(ws * GRID_W, GRID_W)
    nkeys = NA_WIN * GRID_W
    k = k_ref[0, pl.ds(ks, nkeys), :]
    v = v_ref[0, pl.ds(ks, nkeys), :]
    q = q_ref[0]
    lane = lax.broadcasted_iota(jnp.int32, (1, LANES), 1)
    outs = []
    for hd in range(2):
        mine = (lane < NA_HEAD_DIM) if hd == 0 else (lane >= NA_HEAD_DIM)
        s = _dot_nt(jnp.where(mine, q, jnp.zeros_like(q)), k) + bias_ref[hd, 0]
        p = jnp.exp2(s - jnp.max(s, axis=-1, keepdims=True))
        l = jnp.sum(p, axis=-1, keepdims=True)
        outs.append(_dot(p.astype(BF16), v) / l)
    o_ref[0] = jnp.where(lane < NA_HEAD_DIM, outs[0], outs[1]).astype(o_ref.dtype)


def _na_attn(q, k, v, bias):
    B, S, _ = q.shape
    tq = NA_GROUP * GRID_W
    n_groups = S // tq

    def bias_map(p, b, g):
        return (p, (g > 0).astype(jnp.int32) + (g == n_groups - 1).astype(jnp.int32), 0, 0)

    return pl.pallas_call(
        functools.partial(_na_attn_body, n_groups=n_groups),
        grid=(NA_HEADS // 2, B, n_groups),
        in_specs=[pl.BlockSpec((1, tq, LANES), lambda p, b, g: (b, g, p)),
                  pl.BlockSpec((1, S, LANES), lambda p, b, g: (b, 0, p)),
                  pl.BlockSpec((1, S, LANES), lambda p, b, g: (b, 0, p)),
                  pl.BlockSpec((2, 1, tq, NA_WIN * GRID_W), bias_map)],
        out_specs=pl.BlockSpec((1, tq, LANES), lambda p, b, g: (b, g, p)),
        out_shape=jax.ShapeDtypeStruct((B, S, NA_W), BF16),
        compiler_params=_params(("parallel", "parallel", "parallel")),
        name="na_attn",
    )(q, k, v, bias)


def _na_bias_tiles(rpb, rows):
    assert rows >= NA_WIN and rows % NA_GROUP == 0
    half_r = NA_ROWS // 2
    rho = np.arange(NA_GROUP)
    r_rel = np.stack([rho, half_r + rho, NA_ROWS + rho])
    rs_rel = np.stack([0 * rho, rho, np.full_like(rho, half_r)])
    a = np.arange(NA_WIN)
    c = np.arange(GRID_W)
    cs = np.clip(c - NA_COLS // 2, 0, GRID_W - NA_COLS)
    row_ok = (a[None, None, :] >= rs_rel[:, :, None]) & (a[None, None, :] < rs_rel[:, :, None] + NA_ROWS)
    col_ok = (c[None, :] >= cs[:, None]) & (c[None, :] < cs[:, None] + NA_COLS)
    dr = np.clip(a[None, None, :] - r_rel[:, :, None] + NA_ROWS - 1, 0, 2 * NA_ROWS - 2)
    dc = np.clip(c[None, :] - c[:, None] + NA_COLS - 1, 0, 2 * NA_COLS - 2)
    ok = row_ok[:, :, None, :, None] & col_ok[None, None, :, None, :]
    dr_b = np.broadcast_to(dr[:, :, None, :, None], ok.shape)
    dc_b = np.broadcast_to(dc[None, None, :, None, :], ok.shape)
    vals = rpb.astype(F32)[:, dr_b, dc_b]
    tiles = jnp.where(ok[None], vals * LOG2E, NEG_MASK)
    return tiles.reshape(rpb.shape[0], 3, NA_GROUP * GRID_W, NA_WIN * GRID_W)


def _diff_attn_body(q_ref, k_ref, v_ref, bias_ref, lam_ref, subln_ref, o_ref, *, lambda_init):
    t = q_ref.shape[1]
    nk = k_ref.shape[1] // t
    i = pl.program_id(2)
    q = q_ref[0]
    lane = lax.broadcasted_iota(jnp.int32, (1, LANES), 1)
    zero = jnp.zeros_like(q)
    q1 = jnp.where(lane < DIFF_HEAD_DIM, q, zero)
    q2 = jnp.where(lane >= DIFF_HEAD_DIM, q, zero)

    def step(j, carry):
        ks = pl.multiple_of(j * t, t)
        k = k_ref[0, pl.ds(ks, t), :]
        v = v_ref[0, pl.ds(ks, t), :]
        bias = bias_ref[0, jnp.clip(j - i, -2, 2) + 2]
        c1 = _softmax_step(_dot_nt(q1, k) + bias, v, *carry[0])
        c2 = _softmax_step(_dot_nt(q2, k) + bias, v, *carry[1])
        return (c1, c2)

    (_, l1, a1), (_, l2, a2) = lax.fori_loop(0, nk, step, (_softmax_init(t), _softmax_init(t)))
    lp = lam_ref[...]
    lam = (jnp.exp(jnp.sum(lp[0:1] * lp[1:2], keepdims=True))
           - jnp.exp(jnp.sum(lp[2:3] * lp[3:4], keepdims=True)) + lambda_init)
    o = a1 / l1 - lam * (a2 / l2)
    o_ref[0] = (_rms(o, subln_ref[...]) * (1.0 - lambda_init)).astype(o_ref.dtype)


def _diff_attn(q, k, v, bias, lam_params, subln, lambda_init):
    B, S, _ = q.shape
    t = bias.shape[-1]
    qspec = pl.BlockSpec((1, t, LANES), lambda h, b, i: (b, i, h))
    kvspec = pl.BlockSpec((1, S, LANES), lambda h, b, i: (b, 0, h))
    return pl.pallas_call(
        functools.partial(_diff_attn_body, lambda_init=lambda_init),
        grid=(DIFF_HEADS, B, S // t),
        in_specs=[qspec, kvspec, kvspec,
                  pl.BlockSpec((1, 5, t, t), lambda h, b, i: (h, 0, 0, 0)),
                  pl.BlockSpec(lam_params.shape, lambda h, b, i: (0, 0)),
                  pl.BlockSpec(subln.shape, lambda h, b, i: (0, 0))],
        out_specs=qspec,
        out_shape=jax.ShapeDtypeStruct((B, S, DIFF_W), BF16),
        compiler_params=_params(("parallel", "parallel", "parallel")),
        name="diff_attn",
    )(q, k, v, bias, lam_params, subln)


def _t5_bucket(rel):
    nb = T5_BUCKETS // 2
    max_exact = nb // 2
    ret = jnp.where(rel > 0, nb, 0)
    n = jnp.abs(rel)
    nf = jnp.maximum(n, 1).astype(F32)
    large = max_exact + (jnp.log(nf / max_exact) / math.log(T5_MAX_DIST / max_exact)
                         * (nb - max_exact)).astype(jnp.int32)
    large = jnp.minimum(large, nb - 1)
    return ret + jnp.where(n < max_exact, n, large)


def _t5_bias_tiles(t5_table, t):
    assert t >= T5_MAX_DIST
    off = jnp.arange(-2, 3)[:, None, None] * t
    rel = off + jnp.arange(t)[None, None, :] - jnp.arange(t)[None, :, None]
    return jnp.transpose(t5_table.astype(F32)[_t5_bucket(rel)], (3, 0, 1, 2)) * LOG2E


def _c_proj_body(x_ref, g_ref, w_ref, q_ref, k_ref, v_ref):
    h = _rms(x_ref[...], g_ref[...]).astype(BF16)
    q_ref[...] = (_dot(h, w_ref[:, :DIFF_W]) * (DIFF_HEAD_DIM ** -0.5 * LOG2E)).astype(BF16)
    k_ref[...] = _dot(h, w_ref[:, DIFF_W:2 * DIFF_W]).astype(BF16)
    v_ref[...] = _dot(h, w_ref[:, 2 * DIFF_W:]).astype(BF16)


def _c_proj(x2, g, w, tm):
    N, D = x2.shape
    rows = lambda wd: pl.BlockSpec((tm, wd), lambda i: (i, 0))
    full = lambda a: pl.BlockSpec(a.shape, lambda i: (0,) * a.ndim)
    return pl.pallas_call(
        _c_proj_body,
        grid=(N // tm,),
        in_specs=[rows(D), full(g), full(w)],
        out_specs=[rows(DIFF_W)] * 3,
        out_shape=[jax.ShapeDtypeStruct((N, DIFF_W), BF16)] * 3,
        compiler_params=_params(("parallel",)),
        name="c_proj",
    )(x2, g, w)


def _out_proj_body(x_ref, *refs):
    o_ref = refs[-1]
    n = (len(refs) - 1) // 2
    acc = x_ref[...]
    for a_ref, w_ref in zip(refs[:n], refs[n:2 * n]):
        acc = acc + _dot(a_ref[...], w_ref[...])
    o_ref[...] = acc


def _out_proj(x2, lhs, ws, tm):
    N, D = x2.shape
    rows = lambda a: pl.BlockSpec((tm, a.shape[1]), lambda i: (i, 0))
    full = lambda a: pl.BlockSpec(a.shape, lambda i: (0,) * a.ndim)
    return pl.pallas_call(
        _out_proj_body,
        grid=(N // tm,),
        in_specs=[rows(x2)] + [rows(a) for a in lhs] + [full(w) for w in ws],
        out_specs=rows(x2),
        out_shape=jax.ShapeDtypeStruct((N, D), F32),
        compiler_params=_params(("parallel",)),
        name="out_proj",
    )(x2, *lhs, *ws)


def _mlp_body(x_ref, g_ref, w1_ref, w2_ref, *rest, final):
    o_ref, hn_ref, acc_ref = rest[-3:]
    j = pl.program_id(1)

    @pl.when(j == 0)
    def _():
        hn_ref[...] = _rms(x_ref[...], g_ref[...]).astype(BF16)
        acc_ref[...] = jnp.zeros_like(acc_ref)

    a = jnp.maximum(_dot(hn_ref[...], w1_ref[...]), 0.0)
    acc_ref[...] += _dot((a * a).astype(BF16), w2_ref[...])

    @pl.when(j == pl.num_programs(1) - 1)
    def _():
        y = x_ref[...] + acc_ref[...]
        o_ref[...] = _rms(y, rest[0][...]) if final else y


def _mlp(x2, g, w1, w2, final_g, tm, tf):
    N, D = x2.shape
    F = w1.shape[1]
    final = final_g is not None
    in_specs = [pl.BlockSpec((tm, D), lambda i, j: (i, 0)),
                pl.BlockSpec(g.shape, lambda i, j: (0, 0)),
                pl.BlockSpec((D, tf), lambda i, j: (0, j)),
                pl.BlockSpec((tf, D), lambda i, j: (j, 0))]
    args = [x2, g, w1, w2]
    if final:
        in_specs.append(pl.BlockSpec(final_g.shape, lambda i, j: (0, 0)))
        args.append(final_g)
    return pl.pallas_call(
        functools.partial(_mlp_body, final=final),
        grid=(N // tm, F // tf),
        in_specs=in_specs,
        out_specs=pl.BlockSpec((tm, D), lambda i, j: (i, 0)),
        out_shape=jax.ShapeDtypeStruct((N, D), F32),
        scratch_shapes=[pltpu.VMEM((tm, D), BF16), pltpu.VMEM((tm, D), F32)],
        compiler_params=_params(("parallel", "arbitrary")),
        name="mlp",
    )(*args)


def _rot_swap(w):
    half = MLA_ROPE // 2
    return jnp.concatenate([-w[:, half:], w[:, :half]], axis=1)


def _pad_cols(w, left, width):
    return jnp.pad(w, ((0, 0), (left, width - left - w.shape[1])))


def _pack_ab_weights(w_in, w_uq, w_ukv):
    o = np.cumsum([0, MLA_Q_RANK, MLA_KV_RANK, MLA_ROPE, NA_W, NA_W, NA_W])
    w_kpe = w_in[:, o[2]:o[3]]
    win = jnp.concatenate([w_in[:, o[0]:o[2]], w_in[:, o[3]:o[6]],
                           _pad_cols(w_kpe, MLA_NOPE, LANES), _pad_cols(_rot_swap(w_kpe), MLA_NOPE, LANES)], axis=1)
    dq = MLA_NOPE + MLA_ROPE
    q_main, q_swap, k_main = [], [], []
    for hd in range(MLA_HEADS):
        wq = w_uq[:, hd * dq:(hd + 1) * dq]
        q_main.append(_pad_cols(wq, 0, LANES))
        q_swap.append(_pad_cols(_rot_swap(wq[:, MLA_NOPE:]), MLA_NOPE, LANES))
        k_main.append(_pad_cols(w_ukv[:, hd * LANES:hd * LANES + MLA_NOPE], 0, LANES))
    v_cols = [w_ukv[:, hd * LANES + MLA_NOPE:(hd + 1) * LANES] for hd in range(MLA_HEADS)]
    wuq = jnp.concatenate(q_main + q_swap, axis=1)
    wukv = jnp.concatenate(k_main + v_cols, axis=1)
    return win.astype(BF16), wuq.astype(BF16), wukv.astype(BF16)


def _rope_tables(S):
    inv_freq = ROPE_THETA ** (-jnp.arange(0, MLA_ROPE, 2, dtype=F32) / MLA_ROPE)
    ang = jnp.arange(S, dtype=F32)[:, None] * inv_freq[None, :]
    cos2 = jnp.tile(jnp.cos(ang), (1, 2))
    sin2 = jnp.tile(jnp.sin(ang), (1, 2))
    scale = (MLA_NOPE + MLA_ROPE) ** -0.5 * LOG2E
    pad = lambda t: _pad_cols(t, MLA_NOPE, LANES)
    nope_on = _pad_cols(jnp.ones((S, MLA_NOPE), F32), 0, LANES)
    qa = (nope_on + pad(cos2)) * scale
    qb = pad(sin2) * scale
    return qa, qb, pad(cos2), pad(sin2)


def _tile(n, want):
    t = min(n, want)
    assert n % t == 0
    return t


def kernel(x, norm_attn, norm_mlp, ab_w_in, ab_q_norm, ab_w_uq, ab_kv_norm, ab_w_ukv, ab_natten_rpb, ab_w_out, c_w_in, c_lambda_q1, c_lambda_k1, c_lambda_q2, c_lambda_k2, c_subln, c_w_out, t5_bias, mlp_w1, mlp_w2, final_norm):
    B, S, D = x.shape
    N = B * S
    depth = norm_attn.shape[0]
    tm = _tile(S, 512)
    t_diff = _tile(S, 256)
    qa, qb, kc, ks = _rope_tables(S)
    t5_tiles = _t5_bias_tiles(t5_bias, t_diff)
    row = lambda v: v.reshape(1, -1).astype(F32)

    for layer in range(depth):
        i = layer // 2
        g = row(norm_attn[layer])
        if layer % 2 == 0:
            win, wuq, wukv = _pack_ab_weights(ab_w_in[i], ab_w_uq[i], ab_w_ukv[i])
            q, k, v, qna, kna, vna = _ab_proj(x, g, win, row(ab_q_norm[i]), wuq, row(ab_kv_norm[i]), wukv,
                                              qa, qb, kc, ks, tm)
            o_a = _mla_attn(q, k, v, _tile(S, 256), _tile(S, 512))
            o_b = _na_attn(qna, kna, vna, _na_bias_tiles(ab_natten_rpb[i], S // GRID_W))
            w_out = ab_w_out[i].astype(BF16)
            hv = MLA_HEADS * MLA_V
            x2 = _out_proj(x.reshape(N, D), [o_a.reshape(N, hv), o_b.reshape(N, NA_W)],
                           [w_out[:hv], w_out[hv:]], tm)
        else:
            lambda_init = 0.8 - 0.6 * math.exp(-0.3 * layer)
            q, k, v = _c_proj(x.reshape(N, D), g, c_w_in[i].astype(BF16), tm)
            lam_params = jnp.stack([c_lambda_q1[i], c_lambda_k1[i], c_lambda_q2[i], c_lambda_k2[i]]).astype(F32)
            o = _diff_attn(q.reshape(B, S, DIFF_W), k.reshape(B, S, DIFF_W), v.reshape(B, S, DIFF_W),
                           t5_tiles, lam_params, row(c_subln[i]), lambda_init)
            x2 = _out_proj(x.reshape(N, D), [o.reshape(N, DIFF_W)], [c_w_out[i].astype(BF16)], tm)
        final_g = row(final_norm) if layer == depth - 1 else None
        x2 = _mlp(x2, row(norm_mlp[layer]), mlp_w1[layer].astype(BF16), mlp_w2[layer].astype(BF16), final_g,
                  _tile(N, 1024), _tile(mlp_w1.shape[2], 1024))
        x = x2.reshape(B, S, D)
    return x
```

```python
import functools
import math

import numpy as np
import jax
import jax.numpy as jnp
from jax import lax
from jax.experimental import pallas as pl
from jax.experimental.pallas import tpu as pltpu

F32 = jnp.float32
BF16 = jnp.bfloat16

GRID_W = 64
NORM_EPS = 1e-6
MLA_HEADS = 8
MLA_NOPE = 64
MLA_ROPE = 32
MLA_V = 64
MLA_Q_RANK = 384
MLA_KV_RANK = 256
ROPE_THETA = 10000.0
NA_HEADS = 8
NA_HEAD_DIM = 64
NA_ROWS = 8
NA_COLS = 16
NA_W = NA_HEADS * NA_HEAD_DIM
DIFF_HEADS = 8
DIFF_HEAD_DIM = 64
DIFF_W = DIFF_HEADS * 2 * DIFF_HEAD_DIM
T5_BUCKETS = 32
T5_MAX_DIST = 128

LANES = 128
LOG2E = 1.4426950408889634
NEG_MASK = -1e30
VMEM_LIMIT = 56 * 1024 * 1024

NA_GROUP = 4
NA_WIN = NA_GROUP + NA_ROWS

_OFF_CQ = 0
_OFF_CKV = _OFF_CQ + MLA_Q_RANK
_OFF_QNA = _OFF_CKV + MLA_KV_RANK
_OFF_KNA = _OFF_QNA + NA_W
_OFF_VNA = _OFF_KNA + NA_W
_OFF_KPE = _OFF_VNA + NA_W
_OFF_KPESW = _OFF_KPE + LANES
AB_IN_PACKED = _OFF_KPESW + LANES


def _params(sem):
    return pltpu.CompilerParams(dimension_semantics=sem, vmem_limit_bytes=VMEM_LIMIT)


def _rms(xf, g):
    return xf * lax.rsqrt(jnp.mean(xf * xf, axis=-1, keepdims=True) + NORM_EPS) * g


def _dot(a, b):
    return jnp.dot(a, b, preferred_element_type=F32)


def _dot_nt(a, b):
    return lax.dot_general(a, b, (((1,), (1,)), ((), ())), preferred_element_type=F32)


def _ab_proj_body(x_ref, g_ref, win_ref, qn_ref, wuq_ref, kvn_ref, wukv_ref, qa_ref, qb_ref, kc_ref, ks_ref,
                  q_ref, k_ref, v_ref, qna_ref, kna_ref, vna_ref):
    h = _rms(x_ref[0], g_ref[...]).astype(BF16)

    def proj(a, b):
        return _dot(h, win_ref[:, a:b])

    qna_ref[0] = (proj(_OFF_QNA, _OFF_KNA) * (NA_HEAD_DIM ** -0.5 * LOG2E)).astype(BF16)
    kna_ref[0] = proj(_OFF_KNA, _OFF_VNA).astype(BF16)
    vna_ref[0] = proj(_OFF_VNA, _OFF_KPE).astype(BF16)

    cq = _rms(proj(_OFF_CQ, _OFF_CKV), qn_ref[...]).astype(BF16)
    ckv = _rms(proj(_OFF_CKV, _OFF_QNA), kvn_ref[...]).astype(BF16)
    k_pe = proj(_OFF_KPE, _OFF_KPESW) * kc_ref[...] + proj(_OFF_KPESW, AB_IN_PACKED) * ks_ref[...]
    hw = MLA_HEADS * LANES
    q_main = _dot(cq, wuq_ref[:, :hw])
    q_swap = _dot(cq, wuq_ref[:, hw:])
    k_main = _dot(ckv, wukv_ref[:, :hw])
    qa = qa_ref[...]
    qb = qb_ref[...]
    for hd in range(MLA_HEADS):
        sl = slice(hd * LANES, (hd + 1) * LANES)
        q_ref[0, :, sl] = (q_main[:, sl] * qa + q_swap[:, sl] * qb).astype(BF16)
        k_ref[0, :, sl] = (k_main[:, sl] + k_pe).astype(BF16)
    v_ref[0] = _dot(ckv, wukv_ref[:, hw:]).astype(BF16)


def _ab_proj(x, g, win, qn, wuq, kvn, wukv, qa, qb, kc, ks, tm):
    B, S, D = x.shape
    hw = MLA_HEADS * LANES
    vw = MLA_HEADS * MLA_V
    full = lambda a: pl.BlockSpec(a.shape, lambda b, i: (0,) * a.ndim)
    rows = lambda w: pl.BlockSpec((1, tm, w), lambda b, i: (b, i, 0))
    tab = pl.BlockSpec((tm, LANES), lambda b, i: (i, 0))
    out_w = (hw, hw, vw, NA_W, NA_W, NA_W)
    return pl.pallas_call(
        _ab_proj_body,
        grid=(B, S // tm),
        in_specs=[rows(D), full(g), full(win), full(qn), full(wuq), full(kvn), full(wukv), tab, tab, tab, tab],
        out_specs=[rows(w) for w in out_w],
        out_shape=[jax.ShapeDtypeStruct((B, S, w), BF16) for w in out_w],
        compiler_params=_params(("parallel", "parallel")),
        name="ab_proj",
    )(x, g, win, qn, wuq, kvn, wukv, qa, qb, kc, ks)


def _softmax_step(s, v, m, l, acc):
    m_new = jnp.maximum(m, jnp.max(s, axis=-1, keepdims=True))
    alpha = jnp.exp2(m - m_new)
    p = jnp.exp2(s - m_new)
    l = alpha * l + jnp.sum(p, axis=-1, keepdims=True)
    acc = alpha * acc + _dot(p.astype(BF16), v)
    return m_new, l, acc


def _softmax_init(tq):
    return (jnp.full((tq, 1), -jnp.inf, F32), jnp.zeros((tq, 1), F32), jnp.zeros((tq, LANES), F32))


def _mla_attn_body(q_ref, k_ref, v_ref, o_ref, *, tk):
    tq = q_ref.shape[1]
    nk = k_ref.shape[1] // tk
    qs = [q_ref[0, :, hd * LANES:(hd + 1) * LANES] for hd in range(2)]

    def step(j, carry):
        ks = pl.multiple_of(j * tk, tk)
        v = v_ref[0, pl.ds(ks, tk), :]
        out = []
        for hd in range(2):
            k = k_ref[0, pl.ds(ks, tk), hd * LANES:(hd + 1) * LANES]
            out.append(_softmax_step(_dot_nt(qs[hd], k), v, *carry[hd]))
        return tuple(out)

    res = lax.fori_loop(0, nk, step, (_softmax_init(tq), _softmax_init(tq)))
    o0 = res[0][2] / res[0][1]
    o1 = res[1][2] / res[1][1]
    lane = lax.broadcasted_iota(jnp.int32, (1, LANES), 1)
    o_ref[0] = jnp.where(lane < MLA_V, o0, o1).astype(o_ref.dtype)


def _mla_attn(q, k, v, tq, tk):
    B, S, _ = q.shape
    return pl.pallas_call(
        functools.partial(_mla_attn_body, tk=tk),
        grid=(B, MLA_HEADS // 2, S // tq),
        in_specs=[pl.BlockSpec((1, tq, 2 * LANES), lambda b, p, i: (b, i, p)),
                  pl.BlockSpec((1, S, 2 * LANES), lambda b, p, i: (b, 0, p)),
                  pl.BlockSpec((1, S, LANES), lambda b, p, i: (b, 0, p))],
        out_specs=pl.BlockSpec((1, tq, LANES), lambda b, p, i: (b, i, p)),
        out_shape=jax.ShapeDtypeStruct((B, S, MLA_HEADS * MLA_V), BF16),
        compiler_params=_params(("parallel", "parallel", "parallel")),
        name="mla_attn",
    )(q, k, v)


def _na_attn_body(q_ref, k_ref, v_ref, bias_ref, o_ref, *, n_groups):
    g = pl.program_id(2)
    max_ws = n_groups * NA_GROUP - NA_WIN
    ws = jnp.clip(g * NA_GROUP - NA_ROWS // 2, 0, max_ws)
    ks = pl.multiple_of(ws * GRID_W, GRID_W)
    nkeys = NA_WIN * GRID_W
    k = k_ref[0, pl.ds(ks, nkeys), :]
    v = v_ref[0, pl.ds(ks, nkeys), :]
    q = q_ref[0]
    lane = lax.broadcasted_iota(jnp.int32, (1, LANES), 1)
    outs = []
    for hd in range(2):
        mine = (lane < NA_HEAD_DIM) if hd == 0 else (lane >= NA_HEAD_DIM)
        s = _dot_nt(jnp.where(mine, q, jnp.zeros_like(q)), k) + bias_ref[hd, 0]
        p = jnp.exp2(s - jnp.max(s, axis=-1, keepdims=True))
        l = jnp.sum(p, axis=-1, keepdims=True)
        outs.append(_dot(p.astype(BF16), v) / l)
    o_ref[0] = jnp.where(lane < NA_HEAD_DIM, outs[0], outs[1]).astype(o_ref.dtype)


def _na_attn(q, k, v, bias):
    B, S, _ = q.shape
    tq = NA_GROUP * GRID_W
    n_groups = S // tq

    def bias_map(p, b, g):
        return (p, (g > 0).astype(jnp.int32) + (g == n_groups - 1).astype(jnp.int32), 0, 0)

    return pl.pallas_call(
        functools.partial(_na_attn_body, n_groups=n_groups),
        grid=(NA_HEADS // 2, B, n_groups),
        in_specs=[pl.BlockSpec((1, tq, LANES), lambda p, b, g: (b, g, p)),
                  pl.BlockSpec((1, S, LANES), lambda p, b, g: (b, 0, p)),
                  pl.BlockSpec((1, S, LANES), lambda p, b, g: (b, 0, p)),
                  pl.BlockSpec((2, 1, tq, NA_WIN * GRID_W), bias_map)],
        out_specs=pl.BlockSpec((1, tq, LANES), lambda p, b, g: (b, g, p)),
        out_shape=jax.ShapeDtypeStruct((B, S, NA_W), BF16),
        compiler_params=_params(("parallel", "parallel", "parallel")),
        name="na_attn",
    )(q, k, v, bias)


def _toeplitz(w, n):
    length = w.shape[-1]
    assert length == 2 * n + 1
    lead = w.shape[:-1]
    flat = jnp.broadcast_to(w[..., None, :], lead + (n, length)).reshape(lead + (n * length,))
    return flat[..., :n * (length - 1)].reshape(lead + (n, length - 1))[..., n:2 * n]


def _na_bias_tiles(rpb, rows):
    assert rows >= NA_WIN and rows % NA_GROUP == 0
    heads = rpb.shape[0]
    half_r = NA_ROWS // 2
    c = np.arange(GRID_W)
    cs = np.clip(c - NA_COLS // 2, 0, GRID_W - NA_COLS)
    col_ok = (c[None, :] >= cs[:, None]) & (c[None, :] < cs[:, None] + NA_COLS)
    side = GRID_W - NA_COLS + 1
    by_col = _toeplitz(jnp.pad(rpb.astype(F32), ((0, 0), (0, 0), (side, side))), GRID_W)
    by_col = jnp.transpose(jnp.where(col_ok[None, None], by_col * LOG2E, NEG_MASK), (0, 2, 1, 3))
    tiles = []
    for r_rel0, rs_rel0, rs_step in ((0, 0, 0), (half_r, 0, 1), (NA_ROWS, half_r, 0)):
        per_row = []
        for rho in range(NA_GROUP):
            rs_rel = rs_rel0 + rs_step * rho
            d0 = rs_rel - (r_rel0 + rho) + NA_ROWS - 1
            blk = by_col[:, :, d0:d0 + NA_ROWS]
            per_row.append(jnp.pad(blk, ((0, 0), (0, 0), (rs_rel, NA_WIN - NA_ROWS - rs_rel), (0, 0)),
                                   constant_values=NEG_MASK))
        tiles.append(jnp.stack(per_row, axis=1))
    return jnp.stack(tiles, axis=1).reshape(heads, 3, NA_GROUP * GRID_W, NA_WIN * GRID_W)


def _diff_attn_body(q_ref, k_ref, v_ref, bias_ref, lam_ref, subln_ref, o_ref, *, lambda_init):
    t = q_ref.shape[1]
    nk = k_ref.shape[1] // t
    i = pl.program_id(2)
    q = q_ref[0]
    lane = lax.broadcasted_iota(jnp.int32, (1, LANES), 1)
    zero = jnp.zeros_like(q)
    q1 = jnp.where(lane < DIFF_HEAD_DIM, q, zero)
    q2 = jnp.where(lane >= DIFF_HEAD_DIM, q, zero)

    def step(j, carry):
        ks = pl.multiple_of(j * t, t)
        k = k_ref[0, pl.ds(ks, t), :]
        v = v_ref[0, pl.ds(ks, t), :]
        bias = bias_ref[0, jnp.clip(j - i, -2, 2) + 2]
        c1 = _softmax_step(_dot_nt(q1, k) + bias, v, *carry[0])
        c2 = _softmax_step(_dot_nt(q2, k) + bias, v, *carry[1])
        return (c1, c2)

    (_, l1, a1), (_, l2, a2) = lax.fori_loop(0, nk, step, (_softmax_init(t), _softmax_init(t)))
    lp = lam_ref[...]
    lam = (jnp.exp(jnp.sum(lp[0:1] * lp[1:2], keepdims=True))
           - jnp.exp(jnp.sum(lp[2:3] * lp[3:4], keepdims=True)) + lambda_init)
    o = a1 / l1 - lam * (a2 / l2)
    o_ref[0] = (_rms(o, subln_ref[...]) * (1.0 - lambda_init)).astype(o_ref.dtype)


def _diff_attn(q, k, v, bias, lam_params, subln, lambda_init):
    B, S, _ = q.shape
    t = bias.shape[-1]
    qspec = pl.BlockSpec((1, t, LANES), lambda h, b, i: (b, i, h))
    kvspec = pl.BlockSpec((1, S, LANES), lambda h, b, i: (b, 0, h))
    return pl.pallas_call(
        functools.partial(_diff_attn_body, lambda_init=lambda_init),
        grid=(DIFF_HEADS, B, S // t),
        in_specs=[qspec, kvspec, kvspec,
                  pl.BlockSpec((1, 5, t, t), lambda h, b, i: (h, 0, 0, 0)),
                  pl.BlockSpec(lam_params.shape, lambda h, b, i: (0, 0)),
                  pl.BlockSpec(subln.shape, lambda h, b, i: (0, 0))],
        out_specs=qspec,
        out_shape=jax.ShapeDtypeStruct((B, S, DIFF_W), BF16),
        compiler_params=_params(("parallel", "parallel", "parallel")),
        name="diff_attn",
    )(q, k, v, bias, lam_params, subln)


def _t5_bucket(rel):
    nb = T5_BUCKETS // 2
    max_exact = nb // 2
    ret = jnp.where(rel > 0, nb, 0)
    n = jnp.abs(rel)
    nf = jnp.maximum(n, 1).astype(F32)
    large = max_exact + (jnp.log(nf / max_exact) / math.log(T5_MAX_DIST / max_exact)
                         * (nb - max_exact)).astype(jnp.int32)
    large = jnp.minimum(large, nb - 1)
    return ret + jnp.where(n < max_exact, n, large)


def _t5_bias_tiles(t5_table, t):
    assert t >= T5_MAX_DIST
    heads = t5_table.shape[1]
    rel = jnp.arange(-2 * t, 2 * t + 1)
    onehot = _t5_bucket(rel)[:, None, None] == jnp.arange(T5_BUCKETS)[None, :, None]
    by_rel = jnp.sum(jnp.where(onehot, t5_table.astype(F32)[None] * LOG2E, 0.0), axis=1).T
    near = [_toeplitz(by_rel[:, (d + 1) * t:(d + 3) * t + 1], t) for d in (-1, 0, 1)]
    far = lambda col: jnp.broadcast_to(by_rel[:, col, None, None], (heads, t, t))
    return jnp.stack([far(0)] + near + [far(4 * t)], axis=1)


def _c_proj_body(x_ref, g_ref, w_ref, q_ref, k_ref, v_ref):
    h = _rms(x_ref[...], g_ref[...]).astype(BF16)
    q_ref[...] = (_dot(h, w_ref[:, :DIFF_W]) * (DIFF_HEAD_DIM ** -0.5 * LOG2E)).astype(BF16)
    k_ref[...] = _dot(h, w_ref[:, DIFF_W:2 * DIFF_W]).astype(BF16)
    v_ref[...] = _dot(h, w_ref[:, 2 * DIFF_W:]).astype(BF16)


def _c_proj(x2, g, w, tm):
    N, D = x2.shape
    rows = lambda wd: pl.BlockSpec((tm, wd), lambda i: (i, 0))
    full = lambda a: pl.BlockSpec(a.shape, lambda i: (0,) * a.ndim)
    return pl.pallas_call(
        _c_proj_body,
        grid=(N // tm,),
        in_specs=[rows(D), full(g), full(w)],
        out_specs=[rows(DIFF_W)] * 3,
        out_shape=[jax.ShapeDtypeStruct((N, DIFF_W), BF16)] * 3,
        compiler_params=_params(("parallel",)),
        name="c_proj",
    )(x2, g, w)


def _out_proj_body(x_ref, *refs):
    o_ref = refs[-1]
    n = (len(refs) - 1) // 2
    acc = x_ref[...]
    for a_ref, w_ref in zip(refs[:n], refs[n:2 * n]):
        acc = acc + _dot(a_ref[...], w_ref[...])
    o_ref[...] = acc


def _out_proj(x2, lhs, ws, tm):
    N, D = x2.shape
    rows = lambda a: pl.BlockSpec((tm, a.shape[1]), lambda i: (i, 0))
    full = lambda a: pl.BlockSpec(a.shape, lambda i: (0,) * a.ndim)
    return pl.pallas_call(
        _out_proj_body,
        grid=(N // tm,),
        in_specs=[rows(x2)] + [rows(a) for a in lhs] + [full(w) for w in ws],
        out_specs=rows(x2),
        out_shape=jax.ShapeDtypeStruct((N, D), F32),
        compiler_params=_params(("parallel",)),
        name="out_proj",
    )(x2, *lhs, *ws)


def _mlp_body(x_ref, g_ref, w1_ref, w2_ref, *rest, final):
    o_ref, hn_ref, acc_ref = rest[-3:]
    j = pl.program_id(1)

    @pl.when(j == 0)
    def _():
        hn_ref[...] = _rms(x_ref[...], g_ref[...]).astype(BF16)
        acc_ref[...] = jnp.zeros_like(acc_ref)

    a = jnp.maximum(_dot(hn_ref[...], w1_ref[...]), 0.0)
    acc_ref[...] += _dot((a * a).astype(BF16), w2_ref[...])

    @pl.when(j == pl.num_programs(1) - 1)
    def _():
        y = x_ref[...] + acc_ref[...]
        o_ref[...] = _rms(y, rest[0][...]) if final else y


def _mlp(x2, g, w1, w2, final_g, tm, tf):
    N, D = x2.shape
    F = w1.shape[1]
    final = final_g is not None
    in_specs = [pl.BlockSpec((tm, D), lambda i, j: (i, 0)),
                pl.BlockSpec(g.shape, lambda i, j: (0, 0)),
                pl.BlockSpec((D, tf), lambda i, j: (0, j)),
                pl.BlockSpec((tf, D), lambda i, j: (j, 0))]
    args = [x2, g, w1, w2]
    if final:
        in_specs.append(pl.BlockSpec(final_g.shape, lambda i, j: (0, 0)))
        args.append(final_g)
    return pl.pallas_call(
        functools.partial(_mlp_body, final=final),
        grid=(N // tm, F // tf),
        in_specs=in_specs,
        out_specs=pl.BlockSpec((tm, D), lambda i, j: (i, 0)),
        out_shape=jax.ShapeDtypeStruct((N, D), F32),
        scratch_shapes=[pltpu.VMEM((tm, D), BF16), pltpu.VMEM((tm, D), F32)],
        compiler_params=_params(("parallel", "arbitrary")),
        name="mlp",
    )(*args)


def _rot_swap(w):
    half = MLA_ROPE // 2
    return jnp.concatenate([-w[:, half:], w[:, :half]], axis=1)


def _pad_cols(w, left, width):
    return jnp.pad(w, ((0, 0), (left, width - left - w.shape[1])))


def _pack_ab_weights(w_in, w_uq, w_ukv):
    o = np.cumsum([0, MLA_Q_RANK, MLA_KV_RANK, MLA_ROPE, NA_W, NA_W, NA_W])
    w_kpe = w_in[:, o[2]:o[3]]
    win = jnp.concatenate([w_in[:, o[0]:o[2]], w_in[:, o[3]:o[6]],
                           _pad_cols(w_kpe, MLA_NOPE, LANES), _pad_cols(_rot_swap(w_kpe), MLA_NOPE, LANES)], axis=1)
    dq = MLA_NOPE + MLA_ROPE
    q_main, q_swap, k_main = [], [], []
    for hd in range(MLA_HEADS):
        wq = w_uq[:, hd * dq:(hd + 1) * dq]
        q_main.append(_pad_cols(wq, 0, LANES))
        q_swap.append(_pad_cols(_rot_swap(wq[:, MLA_NOPE:]), MLA_NOPE, LANES))
        k_main.append(_pad_cols(w_ukv[:, hd * LANES:hd * LANES + MLA_NOPE], 0, LANES))
    v_cols = [w_ukv[:, hd * LANES + MLA_NOPE:(hd + 1) * LANES] for hd in range(MLA_HEADS)]
    wuq = jnp.concatenate(q_main + q_swap, axis=1)
    wukv = jnp.concatenate(k_main + v_cols, axis=1)
    return win.astype(BF16), wuq.astype(BF16), wukv.astype(BF16)


def _rope_tables(S):
    inv_freq = ROPE_THETA ** (-jnp.arange(0, MLA_ROPE, 2, dtype=F32) / MLA_ROPE)
    ang = jnp.arange(S, dtype=F32)[:, None] * inv_freq[None, :]
    cos2 = jnp.tile(jnp.cos(ang), (1, 2))
    sin2 = jnp.tile(jnp.sin(ang), (1, 2))
    scale = (MLA_NOPE + MLA_ROPE) ** -0.5 * LOG2E
    pad = lambda t: _pad_cols(t, MLA_NOPE, LANES)
    nope_on = _pad_cols(jnp.ones((S, MLA_NOPE), F32), 0, LANES)
    qa = (nope_on + pad(cos2)) * scale
    qb = pad(sin2) * scale
    return qa, qb, pad(cos2), pad(sin2)


def _tile(n, want):
    t = min(n, want)
    assert n % t == 0
    return t


def kernel(x, norm_attn, norm_mlp, ab_w_in, ab_q_norm, ab_w_uq, ab_kv_norm, ab_w_ukv, ab_natten_rpb, ab_w_out, c_w_in, c_lambda_q1, c_lambda_k1, c_lambda_q2, c_lambda_k2, c_subln, c_w_out, t5_bias, mlp_w1, mlp_w2, final_norm):
    B, S, D = x.shape
    N = B * S
    depth = norm_attn.shape[0]
    tm = _tile(S, 512)
    t_diff = _tile(S, 512)
    qa, qb, kc, ks = _rope_tables(S)
    t5_tiles = _t5_bias_tiles(t5_bias, t_diff)
    row = lambda v: v.reshape(1, -1).astype(F32)

    for layer in range(depth):
        i = layer // 2
        g = row(norm_attn[layer])
        if layer % 2 == 0:
            win, wuq, wukv = _pack_ab_weights(ab_w_in[i], ab_w_uq[i], ab_w_ukv[i])
            q, k, v, qna, kna, vna = _ab_proj(x, g, win, row(ab_q_norm[i]), wuq, row(ab_kv_norm[i]), wukv,
                                              qa, qb, kc, ks, tm)
            o_a = _mla_attn(q, k, v, _tile(S, 512), _tile(S, 2048))
            o_b = _na_attn(qna, kna, vna, _na_bias_tiles(ab_natten_rpb[i], S // GRID_W))
            w_out = ab_w_out[i].astype(BF16)
            hv = MLA_HEADS * MLA_V
            x2 = _out_proj(x.reshape(N, D), [o_a.reshape(N, hv), o_b.reshape(N, NA_W)],
                           [w_out[:hv], w_out[hv:]], tm)
        else:
            lambda_init = 0.8 - 0.6 * math.exp(-0.3 * layer)
            q, k, v = _c_proj(x.reshape(N, D), g, c_w_in[i].astype(BF16), tm)
            lam_params = jnp.stack([c_lambda_q1[i], c_lambda_k1[i], c_lambda_q2[i], c_lambda_k2[i]]).astype(F32)
            o = _diff_attn(q.reshape(B, S, DIFF_W), k.reshape(B, S, DIFF_W), v.reshape(B, S, DIFF_W),
                           t5_tiles, lam_params, row(c_subln[i]), lambda_init)
            x2 = _out_proj(x.reshape(N, D), [o.reshape(N, DIFF_W)], [c_w_out[i].astype(BF16)], tm)
        final_g = row(final_norm) if layer == depth - 1 else None
        x2 = _mlp(x2, row(norm_mlp[layer]), mlp_w1[layer].astype(BF16), mlp_w2[layer].astype(BF16), final_g,
                  _tile(N, 1024), _tile(mlp_w1.shape[2], 1024))
        x = x2.reshape(B, S, D)
    return x
```

```python
import functools
import math

import numpy as np
import jax
import jax.numpy as jnp
from jax import lax
from jax.experimental import pallas as pl
from jax.experimental.pallas import tpu as pltpu

F32 = jnp.float32
BF16 = jnp.bfloat16

GRID_W = 64
NORM_EPS = 1e-6
MLA_HEADS = 8
MLA_NOPE = 64
MLA_ROPE = 32
MLA_V = 64
MLA_Q_RANK = 384
MLA_KV_RANK = 256
ROPE_THETA = 10000.0
NA_HEADS = 8
NA_HEAD_DIM = 64
NA_ROWS = 8
NA_COLS = 16
NA_W = NA_HEADS * NA_HEAD_DIM
DIFF_HEADS = 8
DIFF_HEAD_DIM = 64
DIFF_W = DIFF_HEADS * 2 * DIFF_HEAD_DIM
T5_BUCKETS = 32
T5_MAX_DIST = 128

LANES = 128
LOG2E = 1.4426950408889634
NEG_MASK = -1e30
VMEM_LIMIT = 56 * 1024 * 1024

NA_GROUP = 4
NA_WIN = NA_GROUP + NA_ROWS

_OFF_CQ = 0
_OFF_CKV = _OFF_CQ + MLA_Q_RANK
_OFF_QNA = _OFF_CKV + MLA_KV_RANK
_OFF_KNA = _OFF_QNA + NA_W
_OFF_VNA = _OFF_KNA + NA_W
_OFF_KPE = _OFF_VNA + NA_W
_OFF_KPESW = _OFF_KPE + LANES
AB_IN_PACKED = _OFF_KPESW + LANES


def _params(sem):
    return pltpu.CompilerParams(dimension_semantics=sem, vmem_limit_bytes=VMEM_LIMIT)


def _rms(xf, g):
    return xf * lax.rsqrt(jnp.mean(xf * xf, axis=-1, keepdims=True) + NORM_EPS) * g


def _dot(a, b):
    return jnp.dot(a, b, preferred_element_type=F32)


def _dot_nt(a, b):
    return lax.dot_general(a, b, (((1,), (1,)), ((), ())), preferred_element_type=F32)


def _ab_proj_body(x_ref, g_ref, win_ref, qn_ref, wuq_ref, kvn_ref, wukv_ref, qa_ref, qb_ref, kc_ref, ks_ref,
                  q_ref, k_ref, v_ref, qna_ref, kna_ref, vna_ref):
    h = _rms(x_ref[0], g_ref[...]).astype(BF16)

    def proj(a, b):
        return _dot(h, win_ref[:, a:b])

    qna_ref[0] = (proj(_OFF_QNA, _OFF_KNA) * (NA_HEAD_DIM ** -0.5 * LOG2E)).astype(BF16)
    kna_ref[0] = proj(_OFF_KNA, _OFF_VNA).astype(BF16)
    vna_ref[0] = proj(_OFF_VNA, _OFF_KPE).astype(BF16)

    cq = _rms(proj(_OFF_CQ, _OFF_CKV), qn_ref[...]).astype(BF16)
    ckv = _rms(proj(_OFF_CKV, _OFF_QNA), kvn_ref[...]).astype(BF16)
    k_pe = proj(_OFF_KPE, _OFF_KPESW) * kc_ref[...] + proj(_OFF_KPESW, AB_IN_PACKED) * ks_ref[...]
    hw = MLA_HEADS * LANES
    q_main = _dot(cq, wuq_ref[:, :hw])
    q_swap = _dot(cq, wuq_ref[:, hw:])
    k_main = _dot(ckv, wukv_ref[:, :hw])
    qa = qa_ref[...]
    qb = qb_ref[...]
    for hd in range(MLA_HEADS):
        sl = slice(hd * LANES, (hd + 1) * LANES)
        q_ref[0, :, sl] = (q_main[:, sl] * qa + q_swap[:, sl] * qb).astype(BF16)
        k_ref[0, :, sl] = (k_main[:, sl] + k_pe).astype(BF16)
    v_ref[0] = _dot(ckv, wukv_ref[:, hw:]).astype(BF16)


def _ab_proj(x, g, win, qn, wuq, kvn, wukv, qa, qb, kc, ks, tm):
    B, S, D = x.shape
    hw = MLA_HEADS * LANES
    vw = MLA_HEADS * MLA_V
    full = lambda a: pl.BlockSpec(a.shape, lambda b, i: (0,) * a.ndim)
    rows = lambda w: pl.BlockSpec((1, tm, w), lambda b, i: (b, i, 0))
    tab = pl.BlockSpec((tm, LANES), lambda b, i: (i, 0))
    out_w = (hw, hw, vw, NA_W, NA_W, NA_W)
    return pl.pallas_call(
        _ab_proj_body,
        grid=(B, S // tm),
        in_specs=[rows(D), full(g), full(win), full(qn), full(wuq), full(kvn), full(wukv), tab, tab, tab, tab],
        out_specs=[rows(w) for w in out_w],
        out_shape=[jax.ShapeDtypeStruct((B, S, w), BF16) for w in out_w],
        compiler_params=_params(("parallel", "parallel")),
        name="ab_proj",
    )(x, g, win, qn, wuq, kvn, wukv, qa, qb, kc, ks)


def _softmax_step(s, v, m, l, acc):
    m_new = jnp.maximum(m, jnp.max(s, axis=-1, keepdims=True))
    alpha = jnp.exp2(m - m_new)
    p = jnp.exp2(s - m_new)
    l = alpha * l + jnp.sum(p, axis=-1, keepdims=True)
    acc = alpha * acc + _dot(p.astype(BF16), v)
    return m_new, l, acc


def _softmax_init(tq):
    return (jnp.full((tq, 1), -jnp.inf, F32), jnp.zeros((tq, 1), F32), jnp.zeros((tq, LANES), F32))


def _mla_attn_body(q_ref, k_ref, v_ref, o_ref, *, tk):
    tq = q_ref.shape[1]
    nk = k_ref.shape[1] // tk
    qs = [q_ref[0, :, hd * LANES:(hd + 1) * LANES] for hd in range(2)]

    def step(j, carry):
        ks = pl.multiple_of(j * tk, tk)
        v = v_ref[0, pl.ds(ks, tk), :]
        out = []
        for hd in range(2):
            k = k_ref[0, pl.ds(ks, tk), hd * LANES:(hd + 1) * LANES]
            out.append(_softmax_step(_dot_nt(qs[hd], k), v, *carry[hd]))
        return tuple(out)

    res = lax.fori_loop(0, nk, step, (_softmax_init(tq), _softmax_init(tq)))
    o0 = res[0][2] / res[0][1]
    o1 = res[1][2] / res[1][1]
    lane = lax.broadcasted_iota(jnp.int32, (1, LANES), 1)
    o_ref[0] = jnp.where(lane < MLA_V, o0, o1).astype(o_ref.dtype)


def _mla_attn(q, k, v, tq, tk):
    B, S, _ = q.shape
    return pl.pallas_call(
        functools.partial(_mla_attn_body, tk=tk),
        grid=(B, MLA_HEADS // 2, S // tq),
        in_specs=[pl.BlockSpec((1, tq, 2 * LANES), lambda b, p, i: (b, i, p)),
                  pl.BlockSpec((1, S, 2 * LANES), lambda b, p, i: (b, 0, p)),
                  pl.BlockSpec((1, S, LANES), lambda b, p, i: (b, 0, p))],
        out_specs=pl.BlockSpec((1, tq, LANES), lambda b, p, i: (b, i, p)),
        out_shape=jax.ShapeDtypeStruct((B, S, MLA_HEADS * MLA_V), BF16),
        compiler_params=_params(("parallel", "parallel", "parallel")),
        name="mla_attn",
    )(q, k, v)


def _na_attn_body(q_ref, k_ref, v_ref, bias_ref, o_ref, *, n_groups):
    g = pl.program_id(2)
    max_ws = n_groups * NA_GROUP - NA_WIN
    ws = jnp.clip(g * NA_GROUP - NA_ROWS // 2, 0, max_ws)
    ks = pl.multiple_of(ws * GRID_W, GRID_W)
    nkeys = NA_WIN * GRID_W
    k = k_ref[0, pl.ds(ks, nkeys), :]
    v = v_ref[0, pl.ds(ks, nkeys), :]
    q = q_ref[0]
    lane = lax.broadcasted_iota(jnp.int32, (1, LANES), 1)
    outs = []
    for hd in range(2):
        mine = (lane < NA_HEAD_DIM) if hd == 0 else (lane >= NA_HEAD_DIM)
        s = _dot_nt(jnp.where(mine, q, jnp.zeros_like(q)), k) + bias_ref[hd, 0]
        p = jnp.exp2(s - jnp.max(s, axis=-1, keepdims=True))
        l = jnp.sum(p, axis=-1, keepdims=True)
        outs.append(_dot(p.astype(BF16), v) / l)
    o_ref[0] = jnp.where(lane < NA_HEAD_DIM, outs[0], outs[1]).astype(o_ref.dtype)


def _na_attn(q, k, v, bias):
    B, S, _ = q.shape
    tq = NA_GROUP * GRID_W
    n_groups = S // tq

    def bias_map(p, b, g):
        return (p, (g > 0).astype(jnp.int32) + (g == n_groups - 1).astype(jnp.int32), 0, 0)

    return pl.pallas_call(
        functools.partial(_na_attn_body, n_groups=n_groups),
        grid=(NA_HEADS // 2, B, n_groups),
        in_specs=[pl.BlockSpec((1, tq, LANES), lambda p, b, g: (b, g, p)),
                  pl.BlockSpec((1, S, LANES), lambda p, b, g: (b, 0, p)),
                  pl.BlockSpec((1, S, LANES), lambda p, b, g: (b, 0, p)),
                  pl.BlockSpec((2, 1, tq, NA_WIN * GRID_W), bias_map)],
        out_specs=pl.BlockSpec((1, tq, LANES), lambda p, b, g: (b, g, p)),
        out_shape=jax.ShapeDtypeStruct((B, S, NA_W), BF16),
        compiler_params=_params(("parallel", "parallel", "parallel")),
        name="na_attn",
    )(q, k, v, bias)


def _toeplitz(w, n):
    length = w.shape[-1]
    assert length == 2 * n + 1
    lead = w.shape[:-1]
    flat = jnp.broadcast_to(w[..., None, :], lead + (n, length)).reshape(lead + (n * length,))
    return flat[..., :n * (length - 1)].reshape(lead + (n, length - 1))[..., n:2 * n]


def _na_bias_tiles(rpb, rows):
    assert rows >= NA_WIN and rows % NA_GROUP == 0
    heads = rpb.shape[0]
    half_r = NA_ROWS // 2
    c = np.arange(GRID_W)
    cs = np.clip(c - NA_COLS // 2, 0, GRID_W - NA_COLS)
    col_ok = (c[None, :] >= cs[:, None]) & (c[None, :] < cs[:, None] + NA_COLS)
    side = GRID_W - NA_COLS + 1
    by_col = _toeplitz(jnp.pad(rpb.astype(F32), ((0, 0), (0, 0), (side, side))), GRID_W)
    by_col = jnp.transpose(jnp.where(col_ok[None, None], by_col * LOG2E, NEG_MASK), (0, 2, 1, 3))
    tiles = []
    for r_rel0, rs_rel0, rs_step in ((0, 0, 0), (half_r, 0, 1), (NA_ROWS, half_r, 0)):
        per_row = []
        for rho in range(NA_GROUP):
            rs_rel = rs_rel0 + rs_step * rho
            d0 = rs_rel - (r_rel0 + rho) + NA_ROWS - 1
            blk = by_col[:, :, d0:d0 + NA_ROWS]
            per_row.append(jnp.pad(blk, ((0, 0), (0, 0), (rs_rel, NA_WIN - NA_ROWS - rs_rel), (0, 0)),
                                   constant_values=NEG_MASK))
        tiles.append(jnp.stack(per_row, axis=1))
    return jnp.stack(tiles, axis=1).reshape(heads, 3, NA_GROUP * GRID_W, NA_WIN * GRID_W)


DIFF_KEY_BLOCKS = 4


def _softmax_step_t(st, vt, m, l, acc):
    m_new = jnp.maximum(m, jnp.max(st, axis=0, keepdims=True))
    alpha = jnp.exp2(m - m_new)
    p = jnp.exp2(st - m_new)
    l = alpha * l + jnp.sum(p, axis=0, keepdims=True)
    acc = alpha * acc + _dot(vt, p.astype(BF16))
    return m_new, l, acc


def _softmax_init_t(tq):
    return (jnp.full((1, tq), -jnp.inf, F32), jnp.zeros((1, tq), F32), jnp.zeros((LANES, tq), F32))


def _diff_attn_body(qt_ref, k_ref, vt_ref, bias_ref, lam_ref, subln_ref, o_ref, *, lambda_init, kb):
    t = qt_ref.shape[2]
    tk = kb * t
    nk = k_ref.shape[1] // tk
    i = pl.program_id(2)
    qt = qt_ref[0]
    row = lax.broadcasted_iota(jnp.int32, (LANES, 1), 0)
    zero = jnp.zeros_like(qt)
    q1 = jnp.where(row < DIFF_HEAD_DIM, qt, zero)
    q2 = jnp.where(row >= DIFF_HEAD_DIM, qt, zero)

    def step(j, carry):
        ks = pl.multiple_of(j * tk, tk)
        k = k_ref[0, pl.ds(ks, tk), :]
        vt = vt_ref[0, :, pl.ds(ks, tk)]
        bias = jnp.concatenate([bias_ref[0, jnp.clip(j * kb + u - i, -2, 2) + 2] for u in range(kb)], axis=0)
        c1 = _softmax_step_t(_dot(k, q1) + bias, vt, *carry[0])
        c2 = _softmax_step_t(_dot(k, q2) + bias, vt, *carry[1])
        return (c1, c2)

    (_, l1, a1), (_, l2, a2) = lax.fori_loop(0, nk, step, (_softmax_init_t(t), _softmax_init_t(t)))
    lp = lam_ref[...]
    lam = (jnp.exp(jnp.sum(lp[0:1] * lp[1:2], keepdims=True))
           - jnp.exp(jnp.sum(lp[2:3] * lp[3:4], keepdims=True)) + lambda_init)
    o = a1 / l1 - lam * (a2 / l2)
    o = o * lax.rsqrt(jnp.mean(o * o, axis=0, keepdims=True) + NORM_EPS) * subln_ref[...] * (1.0 - lambda_init)
    o_ref[0] = o.T.astype(o_ref.dtype)


def _diff_attn(qt, k, vt, bias_t, lam_params, subln_col, lambda_init):
    B, S, _ = k.shape
    t = bias_t.shape[-1]
    kb = DIFF_KEY_BLOCKS if S % (DIFF_KEY_BLOCKS * t) == 0 else 1
    return pl.pallas_call(
        functools.partial(_diff_attn_body, lambda_init=lambda_init, kb=kb),
        grid=(DIFF_HEADS, B, S // t),
        in_specs=[pl.BlockSpec((1, LANES, t), lambda h, b, i: (b, h, i)),
                  pl.BlockSpec((1, S, LANES), lambda h, b, i: (b, 0, h)),
                  pl.BlockSpec((1, LANES, S), lambda h, b, i: (b, h, 0)),
                  pl.BlockSpec((1, 5, t, t), lambda h, b, i: (h, 0, 0, 0)),
                  pl.BlockSpec(lam_params.shape, lambda h, b, i: (0, 0)),
                  pl.BlockSpec(subln_col.shape, lambda h, b, i: (0, 0))],
        out_specs=pl.BlockSpec((1, t, LANES), lambda h, b, i: (b, i, h)),
        out_shape=jax.ShapeDtypeStruct((B, S, DIFF_W), BF16),
        compiler_params=_params(("parallel", "parallel", "parallel")),
        name="diff_attn",
    )(qt, k, vt, bias_t, lam_params, subln_col)


def _t5_bucket(rel):
    nb = T5_BUCKETS // 2
    max_exact = nb // 2
    ret = jnp.where(rel > 0, nb, 0)
    n = jnp.abs(rel)
    nf = jnp.maximum(n, 1).astype(F32)
    large = max_exact + (jnp.log(nf / max_exact) / math.log(T5_MAX_DIST / max_exact)
                         * (nb - max_exact)).astype(jnp.int32)
    large = jnp.minimum(large, nb - 1)
    return ret + jnp.where(n < max_exact, n, large)


def _t5_bias_tiles(t5_table, t):
    assert t >= T5_MAX_DIST
    rel = (jnp.arange(-2, 3)[:, None, None] * t + jnp.arange(t)[None, :, None] - jnp.arange(t)[None, None, :])
    bucket = _t5_bucket(rel)[None]
    table = t5_table.astype(F32).T * LOG2E
    tiles = jnp.zeros((table.shape[0],) + rel.shape, F32)
    for b in range(T5_BUCKETS):
        tiles = tiles + jnp.where(bucket == b, table[:, b, None, None, None], 0.0)
    return tiles


def _c_proj_body(x_ref, g_ref, wqt_ref, wk_ref, wvt_ref, qt_ref, k_ref, vt_ref):
    h = _rms(x_ref[0], g_ref[...]).astype(BF16)
    qt_ref[0] = (_dot_nt(wqt_ref[...], h) * (DIFF_HEAD_DIM ** -0.5 * LOG2E)).astype(BF16)
    k_ref[0] = _dot(h, wk_ref[...]).astype(BF16)
    vt_ref[0] = _dot_nt(wvt_ref[...], h).astype(BF16)


def _c_proj(x, g, wqt, wk, wvt, tm):
    B, S, D = x.shape
    rows = pl.BlockSpec((1, tm, D), lambda b, i: (b, i, 0))
    cols = pl.BlockSpec((1, DIFF_W, tm), lambda b, i: (b, 0, i))
    full = lambda a: pl.BlockSpec(a.shape, lambda b, i: (0,) * a.ndim)
    return pl.pallas_call(
        _c_proj_body,
        grid=(B, S // tm),
        in_specs=[rows, full(g), full(wqt), full(wk), full(wvt)],
        out_specs=[cols, pl.BlockSpec((1, tm, DIFF_W), lambda b, i: (b, i, 0)), cols],
        out_shape=[jax.ShapeDtypeStruct((B, DIFF_W, S), BF16), jax.ShapeDtypeStruct((B, S, DIFF_W), BF16),
                   jax.ShapeDtypeStruct((B, DIFF_W, S), BF16)],
        compiler_params=_params(("parallel", "parallel")),
        name="c_proj",
    )(x, g, wqt, wk, wvt)


def _out_proj_body(x_ref, *refs):
    o_ref = refs[-1]
    n = (len(refs) - 1) // 2
    acc = x_ref[...]
    for a_ref, w_ref in zip(refs[:n], refs[n:2 * n]):
        acc = acc + _dot(a_ref[...], w_ref[...])
    o_ref[...] = acc


def _out_proj(x2, lhs, ws, tm):
    N, D = x2.shape
    rows = lambda a: pl.BlockSpec((tm, a.shape[1]), lambda i: (i, 0))
    full = lambda a: pl.BlockSpec(a.shape, lambda i: (0,) * a.ndim)
    return pl.pallas_call(
        _out_proj_body,
        grid=(N // tm,),
        in_specs=[rows(x2)] + [rows(a) for a in lhs] + [full(w) for w in ws],
        out_specs=rows(x2),
        out_shape=jax.ShapeDtypeStruct((N, D), F32),
        compiler_params=_params(("parallel",)),
        name="out_proj",
    )(x2, *lhs, *ws)


def _mlp_body(x_ref, g_ref, w1_ref, w2_ref, *rest, final):
    o_ref, hn_ref, acc_ref = rest[-3:]
    j = pl.program_id(1)

    @pl.when(j == 0)
    def _():
        hn_ref[...] = _rms(x_ref[...], g_ref[...]).astype(BF16)
        acc_ref[...] = jnp.zeros_like(acc_ref)

    a = jnp.maximum(_dot(hn_ref[...], w1_ref[...]), 0.0)
    acc_ref[...] += _dot((a * a).astype(BF16), w2_ref[...])

    @pl.when(j == pl.num_programs(1) - 1)
    def _():
        y = x_ref[...] + acc_ref[...]
        o_ref[...] = _rms(y, rest[0][...]) if final else y


def _mlp(x2, g, w1, w2, final_g, tm, tf):
    N, D = x2.shape
    F = w1.shape[1]
    final = final_g is not None
    in_specs = [pl.BlockSpec((tm, D), lambda i, j: (i, 0)),
                pl.BlockSpec(g.shape, lambda i, j: (0, 0)),
                pl.BlockSpec((D, tf), lambda i, j: (0, j)),
                pl.BlockSpec((tf, D), lambda i, j: (j, 0))]
    args = [x2, g, w1, w2]
    if final:
        in_specs.append(pl.BlockSpec(final_g.shape, lambda i, j: (0, 0)))
        args.append(final_g)
    return pl.pallas_call(
        functools.partial(_mlp_body, final=final),
        grid=(N // tm, F // tf),
        in_specs=in_specs,
        out_specs=pl.BlockSpec((tm, D), lambda i, j: (i, 0)),
        out_shape=jax.ShapeDtypeStruct((N, D), F32),
        scratch_shapes=[pltpu.VMEM((tm, D), BF16), pltpu.VMEM((tm, D), F32)],
        compiler_params=_params(("parallel", "arbitrary")),
        name="mlp",
    )(*args)


def _rot_swap(w):
    half = MLA_ROPE // 2
    return jnp.concatenate([-w[:, half:], w[:, :half]], axis=1)


def _pad_cols(w, left, width):
    return jnp.pad(w, ((0, 0), (left, width - left - w.shape[1])))


def _pack_ab_weights(w_in, w_uq, w_ukv):
    o = np.cumsum([0, MLA_Q_RANK, MLA_KV_RANK, MLA_ROPE, NA_W, NA_W, NA_W])
    w_kpe = w_in[:, o[2]:o[3]]
    win = jnp.concatenate([w_in[:, o[0]:o[2]], w_in[:, o[3]:o[6]],
                           _pad_cols(w_kpe, MLA_NOPE, LANES), _pad_cols(_rot_swap(w_kpe), MLA_NOPE, LANES)], axis=1)
    dq = MLA_NOPE + MLA_ROPE
    q_main, q_swap, k_main = [], [], []
    for hd in range(MLA_HEADS):
        wq = w_uq[:, hd * dq:(hd + 1) * dq]
        q_main.append(_pad_cols(wq, 0, LANES))
        q_swap.append(_pad_cols(_rot_swap(wq[:, MLA_NOPE:]), MLA_NOPE, LANES))
        k_main.append(_pad_cols(w_ukv[:, hd * LANES:hd * LANES + MLA_NOPE], 0, LANES))
    v_cols = [w_ukv[:, hd * LANES + MLA_NOPE:(hd + 1) * LANES] for hd in range(MLA_HEADS)]
    wuq = jnp.concatenate(q_main + q_swap, axis=1)
    wukv = jnp.concatenate(k_main + v_cols, axis=1)
    return win.astype(BF16), wuq.astype(BF16), wukv.astype(BF16)


def _rope_tables(S):
    inv_freq = ROPE_THETA ** (-jnp.arange(0, MLA_ROPE, 2, dtype=F32) / MLA_ROPE)
    ang = jnp.arange(S, dtype=F32)[:, None] * inv_freq[None, :]
    cos2 = jnp.tile(jnp.cos(ang), (1, 2))
    sin2 = jnp.tile(jnp.sin(ang), (1, 2))
    scale = (MLA_NOPE + MLA_ROPE) ** -0.5 * LOG2E
    pad = lambda t: _pad_cols(t, MLA_NOPE, LANES)
    nope_on = _pad_cols(jnp.ones((S, MLA_NOPE), F32), 0, LANES)
    qa = (nope_on + pad(cos2)) * scale
    qb = pad(sin2) * scale
    return qa, qb, pad(cos2), pad(sin2)


def _tile(n, want):
    t = min(n, want)
    assert n % t == 0
    return t


def kernel(x, norm_attn, norm_mlp, ab_w_in, ab_q_norm, ab_w_uq, ab_kv_norm, ab_w_ukv, ab_natten_rpb, ab_w_out, c_w_in, c_lambda_q1, c_lambda_k1, c_lambda_q2, c_lambda_k2, c_subln, c_w_out, t5_bias, mlp_w1, mlp_w2, final_norm):
    B, S, D = x.shape
    N = B * S
    depth = norm_attn.shape[0]
    tm = _tile(S, 512)
    t_diff = _tile(S, 512)
    qa, qb, kc, ks = _rope_tables(S)
    t5_tiles = _t5_bias_tiles(t5_bias, t_diff)
    row = lambda v: v.reshape(1, -1).astype(F32)

    for layer in range(depth):
        i = layer // 2
        g = row(norm_attn[layer])
        if layer % 2 == 0:
            win, wuq, wukv = _pack_ab_weights(ab_w_in[i], ab_w_uq[i], ab_w_ukv[i])
            q, k, v, qna, kna, vna = _ab_proj(x, g, win, row(ab_q_norm[i]), wuq, row(ab_kv_norm[i]), wukv,
                                              qa, qb, kc, ks, tm)
            o_a = _mla_attn(q, k, v, _tile(S, 512), _tile(S, 2048))
            o_b = _na_attn(qna, kna, vna, _na_bias_tiles(ab_natten_rpb[i], S // GRID_W))
            w_out = ab_w_out[i].astype(BF16)
            hv = MLA_HEADS * MLA_V
            x2 = _out_proj(x.reshape(N, D), [o_a.reshape(N, hv), o_b.reshape(N, NA_W)],
                           [w_out[:hv], w_out[hv:]], tm)
        else:
            lambda_init = 0.8 - 0.6 * math.exp(-0.3 * layer)
            w = c_w_in[i].astype(BF16)
            qt, k, vt = _c_proj(x, g, w[:, :DIFF_W].T, w[:, DIFF_W:2 * DIFF_W], w[:, 2 * DIFF_W:].T, tm)
            lam_params = jnp.stack([c_lambda_q1[i], c_lambda_k1[i], c_lambda_q2[i], c_lambda_k2[i]]).astype(F32)
            o = _diff_attn(qt, k, vt, t5_tiles, lam_params, c_subln[i].reshape(-1, 1).astype(F32), lambda_init)
            x2 = _out_proj(x.reshape(N, D), [o.reshape(N, DIFF_W)], [c_w_out[i].astype(BF16)], tm)
        final_g = row(final_norm) if layer == depth - 1 else None
        x2 = _mlp(x2, row(norm_mlp[layer]), mlp_w1[layer].astype(BF16), mlp_w2[layer].astype(BF16), final_g,
                  _tile(N, 1024), _tile(mlp_w1.shape[2], 1024))
        x = x2.reshape(B, S, D)
    return x
```
